```python
import math
import functools
import jax
import jax.numpy as jnp
from jax import lax
import numpy as np

D_MODEL = 1024
BATCH = 4
SEQ = 4096
DEPTH = 4

CTX_LEN = 256
GRID_W = 64
Q_BLOCK = 128
CHUNK = 128
ROPE_BASE = 10000.0
EPS = 1e-5
DEEPNORM_ALPHA = (2 * DEPTH) ** 0.25
DEEPNORM_BETA = (8 * DEPTH) ** -0.25
N_EVEN = (DEPTH + 1) // 2
N_ODD = DEPTH // 2
HA = 4
DKA = 64
DVA = 2 * DKA
HB = 4
DKB = 128
DVB = 128
CONV_W = 3
HC = 4
DKC = 64
DVC = 128
HD = 8
HKV = 2
DHD = 64
EVEN_SPLITS = (HA * 2 * DKA, HA * 2 * DKA, HA * DVA, HB * DKB, HB * DKB, HB * DVB, HB * DVB, 4 * HB)
ODD_SPLITS = (HC * DKC, HC * DKC, HC * DVC, HC * DVC, HD * DHD, HKV * DHD, HKV * DHD)
EVEN_IN = sum(EVEN_SPLITS)
ODD_IN = sum(ODD_SPLITS)
EVEN_OUT = HA * DVA + HB * DVB
ODD_OUT = HC * DVC + HD * DHD
N_GROUPS = 4
EXP_PER_GROUP = 8
N_EXPERTS = N_GROUPS * EXP_PER_GROUP
TOP_K = 2
D_EXPERT = 512
MOE_BLOCK = 128

kernel_name = "hybrid_diff_mlstm_retention_gqa_hmoe_prefix"

F32 = jnp.float32


def split_cols(p, sizes):
    return jnp.split(p, np.cumsum(sizes)[:-1].tolist(), axis=-1)


def to_heads(x, n_heads):
    b, t, _ = x.shape
    return x.reshape(b, t, n_heads, -1).transpose(0, 2, 1, 3)


def from_heads(x):
    b, h, t, d = x.shape
    return x.transpose(0, 2, 1, 3).reshape(b, t, h * d)


def layer_norm(x, g, b):
    xf = x.astype(F32)
    mu = xf.mean(-1, keepdims=True)
    var = jnp.square(xf - mu).mean(-1, keepdims=True)
    return ((xf - mu) * lax.rsqrt(var + EPS) * g + b).astype(x.dtype)


def rms_norm(x, g=None):
    xf = x.astype(F32)
    y = xf * lax.rsqrt(jnp.square(xf).mean(-1, keepdims=True) + EPS)
    if g is not None:
        y = y * g
    return y.astype(x.dtype)


def head_layer_norm(x, g, n_heads):
    b, t, w = x.shape
    xf = x.astype(F32).reshape(b, t, n_heads, w // n_heads)
    mu = xf.mean(-1, keepdims=True)
    var = jnp.square(xf - mu).mean(-1, keepdims=True)
    y = ((xf - mu) * lax.rsqrt(var + EPS)).reshape(b, t, w) * g
    return y.astype(x.dtype)


def axial_rope_tables(n_tokens, dim):
    n_rows = n_tokens // GRID_W
    row = jnp.repeat(jnp.arange(n_rows, dtype=F32), GRID_W)
    col = jnp.tile(jnp.arange(GRID_W, dtype=F32), n_rows)
    quarter = dim // 4
    inv_freq = ROPE_BASE ** (-jnp.arange(quarter, dtype=F32) / quarter)
    ang = jnp.concatenate([row[:, None] * inv_freq, col[:, None] * inv_freq], -1)
    return jnp.cos(ang), jnp.sin(ang)


def apply_rope(x, cos, sin):
    half = x.shape[-1] // 2
    x1, x2 = x[..., :half], x[..., half:]
    cos = cos.astype(x.dtype)
    sin = sin.astype(x.dtype)
    return jnp.concatenate([x1 * cos - x2 * sin, x1 * sin + x2 * cos], -1)


def map_query_blocks(block_fn, qs, t_axis):
    t = qs[0].shape[t_axis]
    nb = t // Q_BLOCK

    def split(q):
        q = q.reshape(q.shape[:t_axis] + (nb, Q_BLOCK) + q.shape[t_axis + 1:])
        return jnp.moveaxis(q, t_axis, 0)

    out = lax.map(lambda blk: block_fn(*blk), tuple(split(q) for q in qs))
    out = jnp.moveaxis(out, 0, t_axis)
    return out.reshape(out.shape[:t_axis] + (t,) + out.shape[t_axis + 2:])


def diff_attention(q1, q2, k1, k2, v, lam):
    scale = DKA ** -0.5

    def block(q1b, q2b):
        p1 = jax.nn.softmax(jnp.einsum('bhqd,bhkd->bhqk', q1b, k1).astype(F32) * scale, -1)
        p2 = jax.nn.softmax(jnp.einsum('bhqd,bhkd->bhqk', q2b, k2).astype(F32) * scale, -1)
        return jnp.einsum('bhqk,bhkd->bhqd', (p1 - lam * p2).astype(v.dtype), v)

    return map_query_blocks(block, (q1, q2), 2)


def gqa_attention(q, k, v):
    scale = DHD ** -0.5

    def block(qb):
        s = jnp.einsum('bkgqd,bksd->bkgqs', qb, k).astype(F32) * scale
        p = jax.nn.softmax(s, -1)
        return jnp.einsum('bkgqs,bksd->bkgqd', p.astype(v.dtype), v)

    return map_query_blocks(block, (q,), 3)


def short_conv(x, w, b):
    pad = CONV_W // 2
    t = x.shape[1]
    xp = jnp.pad(x, ((0, 0), (pad, pad), (0, 0)))
    y = b
    for j in range(CONV_W):
        y = y + xp[:, j:j + t] * w[j]
    return y


def to_chunks(a):
    nc = a.shape[2] // CHUNK
    return jnp.moveaxis(a.reshape(a.shape[:2] + (nc, CHUNK) + a.shape[3:]), 2, 0)


def from_chunks(h):
    h = jnp.moveaxis(h, 0, 2)
    return h.reshape(h.shape[:2] + (h.shape[2] * h.shape[3],) + h.shape[4:])


def mlstm_scan(inputs, state, with_out):
    causal = jnp.tril(jnp.ones((CHUNK, CHUNK), bool))

    def body(carry, inp):
        c_mat, n_vec, m = carry
        qc, kc, vc, lic, lfc = inp
        b = jnp.cumsum(lfc, -1)
        b_end = b[..., -1]
        log_end = b_end[..., None] - b + lic
        m_new = jnp.maximum(b_end + m, log_end.max(-1))
        carry_w = jnp.exp(b_end + m - m_new)
        wk = jnp.exp(log_end - m_new[..., None])
        c_new = carry_w[..., None, None] * c_mat + jnp.einsum('bhs,bhse,bhsd->bhed', wk, vc, kc)
        n_new = carry_w[..., None] * n_vec + jnp.einsum('bhs,bhsd->bhd', wk, kc)
        if not with_out:
            return (c_new, n_new, m_new), None
        log_intra = jnp.where(causal, b[..., :, None] - b[..., None, :] + lic[..., None, :], -jnp.inf)
        log_inter = b + m[..., None]
        m_t = jnp.maximum(log_inter, log_intra.max(-1))
        w_intra = jnp.exp(log_intra - m_t[..., None])
        w_inter = jnp.exp(log_inter - m_t)
        s = jnp.einsum('bhtd,bhsd->bhts', qc, kc) * w_intra
        num = w_inter[..., None] * jnp.einsum('bhed,bhtd->bhte', c_mat, qc) + jnp.einsum('bhts,bhse->bhte', s, vc)
        den = w_inter * jnp.einsum('bhd,bhtd->bht', n_vec, qc) + s.sum(-1)
        h = num / jnp.maximum(jnp.abs(den), jnp.exp(-m_t))[..., None]
        return (c_new, n_new, m_new), h

    state, h = lax.scan(body, state, tuple(to_chunks(a) for a in inputs))
    return state, (from_chunks(h) if with_out else None)


def retention_scan(inputs, state, with_out, log_decay):
    pos = jnp.arange(CHUNK, dtype=F32)
    lag = pos[:, None] - pos[None, :]
    lg = log_decay[:, None, None]
    intra = jnp.where(lag >= 0, jnp.exp(jnp.maximum(lag, 0.0) * lg), 0.0)
    q_decay = jnp.exp((pos + 1.0) * log_decay[:, None])[..., None]
    k_decay = jnp.exp((CHUNK - 1.0 - pos) * log_decay[:, None])[..., None]
    chunk_decay = jnp.exp(CHUNK * log_decay)[:, None, None]

    def body(s_mat, inp):
        qc, kc, vc = inp
        s_new = chunk_decay * s_mat + jnp.einsum('bhsd,bhse->bhde', kc * k_decay, vc)
        if not with_out:
            return s_new, None
        scores = jnp.einsum('bhtd,bhsd->bhts', qc, kc) * intra
        o = jnp.einsum('bhts,bhse->bhte', scores, vc) + jnp.einsum('bhtd,bhde->bhte', qc * q_decay, s_mat)
        return s_new, o

    state, o = lax.scan(body, state, tuple(to_chunks(a) for a in inputs))
    return state, (from_chunks(o) if with_out else None)


def time_flip(a, reverse):
    return jnp.flip(a, axis=2) if reverse else a


def bidir_prefix_scan(scan_fns, ctx_dirs, lat_dirs, state0, with_ctx_out):
    h_ctx, h_lat = [], []
    for d in range(2):
        rev = d == 1
        st, oc = scan_fns[d](tuple(time_flip(a, rev) for a in ctx_dirs[d]), state0, with_ctx_out)
        _, ol = scan_fns[d](tuple(time_flip(a, rev) for a in lat_dirs[d]), st, True)
        h_lat.append(time_flip(ol, rev))
        if with_ctx_out:
            h_ctx.append(time_flip(oc, rev))
    return (h_ctx[0] + h_ctx[1] if with_ctx_out else None), h_lat[0] + h_lat[1]


def even_mixer(h_lat, h_ctx, w_in, w_out, lam_p, subln_g, conv_w, conv_b, gate_b, mnorm_g, layer_idx, with_ctx):
    bsz, t_lat, _ = h_lat.shape
    pl = split_cols(h_lat @ w_in, EVEN_SPLITS)
    pc = split_cols(h_ctx @ w_in, EVEN_SPLITS)
    lam_init = 0.8 - 0.6 * math.exp(-0.3 * layer_idx)
    lp = lam_p.astype(F32)
    lam = jnp.exp(jnp.sum(lp[0] * lp[1])) - jnp.exp(jnp.sum(lp[2] * lp[3])) + lam_init
    cos, sin = axial_rope_tables(t_lat, DKA)

    def diff_pair(a):
        b, t, _ = a.shape
        a = a.reshape(b, t, HA, 2, DKA).transpose(3, 0, 2, 1, 4)
        return a[0], a[1]

    def diff_post(o):
        return from_heads(rms_norm(o, subln_g) * (1.0 - lam_init))

    k1_c, k2_c = diff_pair(pc[1])
    v_c = to_heads(pc[2], HA)
    q1_l, q2_l = [apply_rope(a, cos, sin) for a in diff_pair(pl[0])]
    k1_l, k2_l = [apply_rope(a, cos, sin) for a in diff_pair(pl[1])]
    a_l = diff_post(diff_attention(q1_l, q2_l,
                                   jnp.concatenate([k1_c, k1_l], 2),
                                   jnp.concatenate([k2_c, k2_l], 2),
                                   jnp.concatenate([v_c, to_heads(pl[2], HA)], 2), lam))

    def mlstm_dirs(p):
        b, t, _ = p[3].shape
        qk = jax.nn.silu(short_conv(jnp.concatenate([p[3], p[4]], -1), conv_w, conv_b))
        q = to_heads(qk[..., :HB * DKB], HB).astype(F32)
        k = to_heads(qk[..., HB * DKB:], HB).astype(F32) * DKB ** -0.5
        v = to_heads(p[5], HB).astype(F32)
        g = (p[7].reshape(b, t, 4, HB).astype(F32) + gate_b).transpose(2, 0, 3, 1)
        return [(q, k, v, g[0], jax.nn.log_sigmoid(g[1])), (q, k, v, g[2], jax.nn.log_sigmoid(g[3]))]

    state0 = (jnp.zeros((bsz, HB, DVB, DKB), F32), jnp.zeros((bsz, HB, DKB), F32), jnp.zeros((bsz, HB), F32))
    m_c, m_l = bidir_prefix_scan([mlstm_scan, mlstm_scan], mlstm_dirs(pc), mlstm_dirs(pl), state0, with_ctx)

    def mlstm_post(h_sum, p):
        y = from_heads(h_sum).astype(p[6].dtype) * jax.nn.sigmoid(p[6])
        return head_layer_norm(y, mnorm_g, HB)

    out_l = jnp.concatenate([a_l, mlstm_post(m_l, pl)], -1) @ w_out
    out_c = None
    if with_ctx:
        q1_c, q2_c = diff_pair(pc[0])
        a_c = diff_post(diff_attention(q1_c, q2_c, k1_c, k2_c, v_c, lam))
        out_c = jnp.concatenate([a_c, mlstm_post(m_c, pc)], -1) @ w_out
    return out_l, out_c


def odd_mixer(h_lat, h_ctx, w_in, w_out, decay_logit, qk_g, with_ctx):
    bsz, t_lat, _ = h_lat.shape
    pl = split_cols(h_lat @ w_in, ODD_SPLITS)
    pc = split_cols(h_ctx @ w_in, ODD_SPLITS)
    log_decay = jax.nn.log_sigmoid(decay_logit.astype(F32))

    def ret_inputs(p):
        q = to_heads(p[0], HC).astype(F32)
        k = to_heads(p[1], HC).astype(F32) * DKC ** -0.5
        v = to_heads(p[2], HC).astype(F32)
        return (q, k, v)

    ret_c, ret_l = ret_inputs(pc), ret_inputs(pl)
    scan_fns = [functools.partial(retention_scan, log_decay=log_decay[d]) for d in range(2)]
    s0 = jnp.zeros((bsz, HC, DKC, DVC), F32)
    r_c, r_l = bidir_prefix_scan(scan_fns, [ret_c, ret_c], [ret_l, ret_l], s0, with_ctx)

    def ret_post(r, p):
        return from_heads(rms_norm(r)).astype(p[3].dtype) * jax.nn.silu(p[3])

    cos, sin = axial_rope_tables(t_lat, DHD)

    def group_q(q):
        b, h, t, d = q.shape
        return q.reshape(b, HKV, HD // HKV, t, d)

    def ungroup(o):
        b, k, g, t, d = o.shape
        return from_heads(o.reshape(b, k * g, t, d))

    k_c = rms_norm(to_heads(pc[5], HKV), qk_g[1])
    v_c = to_heads(pc[6], HKV)
    q_l = apply_rope(rms_norm(to_heads(pl[4], HD), qk_g[0]), cos, sin)
    k_l = apply_rope(rms_norm(to_heads(pl[5], HKV), qk_g[1]), cos, sin)
    a_l = ungroup(gqa_attention(group_q(q_l), jnp.concatenate([k_c, k_l], 2),
                                jnp.concatenate([v_c, to_heads(pl[6], HKV)], 2)))
    out_l = jnp.concatenate([ret_post(r_l, pl), a_l], -1) @ w_out
    out_c = None
    if with_ctx:
        q_c = rms_norm(to_heads(pc[4], HD), qk_g[0])
        a_c = ungroup(gqa_attention(group_q(q_c), k_c, v_c))
        out_c = jnp.concatenate([ret_post(r_c, pc), a_c], -1) @ w_out
    return out_l, out_c


def hier_moe(h, w_group, b_group, w_router, b_router, w_gate, w_up, w_down):
    n, d = h.shape
    hf = h.astype(F32)
    g_logits = hf @ w_group + b_group
    g_prob = jax.nn.softmax(g_logits, -1)
    g_idx = jnp.argmax(g_logits, -1)
    g_w = jnp.take_along_axis(g_prob, g_idx[:, None], -1)
    e_logits = (hf @ w_router + b_router).reshape(n, N_GROUPS, EXP_PER_GROUP)
    e_logits = jnp.take_along_axis(e_logits, g_idx[:, None, None], 1)[:, 0]
    top_v, top_i = lax.top_k(e_logits, TOP_K)
    weights = (jax.nn.softmax(top_v, -1) * g_w).reshape(-1)
    expert = (g_idx[:, None] * EXP_PER_GROUP + top_i).reshape(-1)
    token = jnp.repeat(jnp.arange(n), TOP_K)
    order = jnp.argsort(expert)
    s_exp, s_tok, s_w = expert[order], token[order], weights[order]
    counts = jnp.bincount(expert, length=N_EXPERTS)
    padded = (counts + MOE_BLOCK - 1) // MOE_BLOCK * MOE_BLOCK
    padded_end = jnp.cumsum(padded)
    dest = (padded_end - padded)[s_exp] + jnp.arange(n * TOP_K) - (jnp.cumsum(counts) - counts)[s_exp]
    n_blocks = -(-(n * TOP_K) // MOE_BLOCK) + N_EXPERTS
    buf = jnp.zeros((n_blocks * MOE_BLOCK, d), h.dtype).at[dest].set(h[s_tok])
    block_exp = jnp.minimum(jnp.searchsorted(padded_end, jnp.arange(n_blocks) * MOE_BLOCK, side='right'),
                            N_EXPERTS - 1)

    def expert_block(args):
        xb, e = args
        return (jax.nn.silu(xb @ w_gate[e]) * (xb @ w_up[e])) @ w_down[e]

    y = lax.map(expert_block, (buf.reshape(n_blocks, MOE_BLOCK, d), block_exp)).reshape(-1, d)
    out = jax.ops.segment_sum(y[dest].astype(F32) * s_w[:, None], s_tok, num_segments=n)
    return out.astype(h.dtype)


def setup_inputs(seed: int = 0) -> dict:
    key = jax.random.key(seed)
    keys = iter(jax.random.split(key, 40))

    def nrm(shape, scale):
        return jax.random.normal(next(keys), shape, F32) * scale

    d = D_MODEL
    forget_base = jnp.zeros((4, HB), F32).at[1].set(jnp.linspace(3.0, 6.0, HB)).at[3].set(jnp.linspace(3.0, 6.0, HB))
    decay_base = jnp.log(2.0 ** (5.0 + jnp.arange(HC, dtype=F32)) - 1.0)
    return {
        "x": nrm((BATCH, SEQ, d), 1.0),
        "c": nrm((BATCH, d), 1.0),
        "ctx": nrm((BATCH, CTX_LEN, d), 1.0),
        "c_ctx": nrm((d,), 1.0),
        "w_mod": nrm((DEPTH, d, 6 * d), 0.5 * d ** -0.5),
        "b_mod": nrm((DEPTH, 6 * d), 0.02),
        "ln_g": 1.0 + nrm((DEPTH, 2, d), 0.02),
        "ln_b": nrm((DEPTH, 2, d), 0.02),
        "ev_w_in": nrm((N_EVEN, d, EVEN_IN), d ** -0.5),
        "ev_w_out": nrm((N_EVEN, EVEN_OUT, d), DEEPNORM_BETA * EVEN_OUT ** -0.5),
        "ev_lambda": nrm((N_EVEN, 4, DKA), 0.1),
        "ev_subln_g": 1.0 + nrm((N_EVEN, DVA), 0.02),
        "ev_conv_w": nrm((N_EVEN, CONV_W, 2 * HB * DKB), CONV_W ** -0.5),
        "ev_conv_b": nrm((N_EVEN, 2 * HB * DKB), 0.02),
        "ev_gate_b": forget_base + nrm((N_EVEN, 4, HB), 0.1),
        "ev_mnorm_g": 1.0 + nrm((N_EVEN, HB * DVB), 0.02),
        "od_w_in": nrm((N_ODD, d, ODD_IN), d ** -0.5),
        "od_w_out": nrm((N_ODD, ODD_OUT, d), DEEPNORM_BETA * ODD_OUT ** -0.5),
        "od_decay": decay_base + nrm((N_ODD, 2, HC), 0.1),
        "od_qk_g": 1.0 + nrm((N_ODD, 2, DHD), 0.02),
        "moe_w_group": nrm((DEPTH, d, N_GROUPS), d ** -0.5),
        "moe_b_group": nrm((DEPTH, N_GROUPS), 0.01),
        "moe_w_router": nrm((DEPTH, d, N_EXPERTS), d ** -0.5),
        "moe_b_router": nrm((DEPTH, N_EXPERTS), 0.01),
        "moe_w_gate": nrm((DEPTH, N_EXPERTS, d, D_EXPERT), d ** -0.5),
        "moe_w_up": nrm((DEPTH, N_EXPERTS, d, D_EXPERT), d ** -0.5),
        "moe_w_down": nrm((DEPTH, N_EXPERTS, D_EXPERT, d), DEEPNORM_BETA * D_EXPERT ** -0.5),
    }


def reference(x, c, ctx, c_ctx, w_mod, b_mod, ln_g, ln_b, ev_w_in, ev_w_out, ev_lambda, ev_subln_g,
              ev_conv_w, ev_conv_b, ev_gate_b, ev_mnorm_g, od_w_in, od_w_out, od_decay, od_qk_g,
              moe_w_group, moe_b_group, moe_w_router, moe_b_router, moe_w_gate, moe_w_up, moe_w_down):
    c_act = jax.nn.silu(c)
    cc_act = jax.nn.silu(c_ctx)
    x_lat, x_ctx = x, ctx
    for layer in range(DEPTH):
        with_ctx = layer < DEPTH - 1
        i = layer // 2
        mod_l = [m[:, None, :] for m in jnp.split(c_act @ w_mod[layer] + b_mod[layer], 6, -1)]
        mod_c = jnp.split(cc_act @ w_mod[layer] + b_mod[layer], 6, -1)
        h_l = x_lat * (1.0 + mod_l[1]) + mod_l[0]
        h_c = x_ctx * (1.0 + mod_c[1]) + mod_c[0]
        if layer % 2 == 0:
            o_l, o_c = even_mixer(h_l, h_c, ev_w_in[i], ev_w_out[i], ev_lambda[i], ev_subln_g[i], ev_conv_w[i],
                                  ev_conv_b[i], ev_gate_b[i], ev_mnorm_g[i], layer, with_ctx)
        else:
            o_l, o_c = odd_mixer(h_l, h_c, od_w_in[i], od_w_out[i], od_decay[i], od_qk_g[i], with_ctx)
        x_lat = layer_norm(DEEPNORM_ALPHA * x_lat + mod_l[2] * o_l, ln_g[layer, 0], ln_b[layer, 0])
        h_l = x_lat * (1.0 + mod_l[4]) + mod_l[3]
        moe_p = (moe_w_group[layer], moe_b_group[layer], moe_w_router[layer], moe_b_router[layer],
                 moe_w_gate[layer], moe_w_up[layer], moe_w_down[layer])
        if with_ctx:
            x_ctx = layer_norm(DEEPNORM_ALPHA * x_ctx + mod_c[2] * o_c, ln_g[layer, 0], ln_b[layer, 0])
            h_c = x_ctx * (1.0 + mod_c[4]) + mod_c[3]
            n_c = h_c.shape[0] * h_c.shape[1]
            y = hier_moe(jnp.concatenate([h_c.reshape(n_c, -1), h_l.reshape(-1, h_l.shape[-1])], 0), *moe_p)
            x_ctx = layer_norm(DEEPNORM_ALPHA * x_ctx + mod_c[5] * y[:n_c].reshape(h_c.shape),
                               ln_g[layer, 1], ln_b[layer, 1])
            y_l = y[n_c:].reshape(h_l.shape)
        else:
            y_l = hier_moe(h_l.reshape(-1, h_l.shape[-1]), *moe_p).reshape(h_l.shape)
        x_lat = layer_norm(DEEPNORM_ALPHA * x_lat + mod_l[5] * y_l, ln_g[layer, 1], ln_b[layer, 1])
    return x_lat
```

```python
import functools
import math

import jax
import jax.numpy as jnp
import numpy as np
from jax import lax
from jax.experimental import pallas as pl
from jax.experimental.pallas import tpu as pltpu

F32 = jnp.float32
BF16 = jnp.bfloat16

GRID_W = 64
ROPE_BASE = 10000.0
EPS = 1e-5
CHUNK = 128
HEAD_A = 4
HEAD_B = 4
HEAD_C = 4
HEAD_D = 8
HEAD_KV = 2
QK_DIM = 64
N_GROUPS = 4
EXP_PER_GROUP = 8
N_EXPERTS = N_GROUPS * EXP_PER_GROUP
TOP_K = 2
MOE_BLOCK = 128
ROW_TILE = 256
LANES = 128
VMEM_LIMIT = 56 * 1024 * 1024


def _cparams(sem):
    return pltpu.CompilerParams(dimension_semantics=sem, vmem_limit_bytes=VMEM_LIMIT)


def _sigmoid(x):
    return 1.0 / (1.0 + jnp.exp(-x))


def _log_sigmoid(x):
    return jnp.minimum(x, 0.0) - jnp.log(1.0 + jnp.exp(-jnp.abs(x)))


def _split3(x):
    hi = x.astype(BF16)
    r = x - hi.astype(F32)
    mid = r.astype(BF16)
    lo = (r - mid.astype(F32)).astype(BF16)
    return hi, mid, lo


def _mod_kernel(c_ref, w_ref, b_ref, o_ref):
    c = c_ref[...]
    act = c * _sigmoid(c)
    o_ref[0] = jnp.dot(act.astype(BF16), w_ref[0].astype(BF16), preferred_element_type=F32) + b_ref[0]


def _modulation(c_rows, w_mod, b_mod):
    depth, d, n = w_mod.shape
    tn = 1024
    return pl.pallas_call(
        _mod_kernel,
        grid=(depth, n // tn),
        in_specs=[
            pl.BlockSpec((8, d), lambda l, j: (0, 0)),
            pl.BlockSpec((1, d, tn), lambda l, j: (l, 0, j)),
            pl.BlockSpec((1, 1, tn), lambda l, j: (l, 0, j)),
        ],
        out_specs=pl.BlockSpec((1, 8, tn), lambda l, j: (l, 0, j)),
        out_shape=jax.ShapeDtypeStruct((depth, 8, n), F32),
        compiler_params=_cparams(("parallel", "parallel")),
        name="adaln_mod",
    )(c_rows, w_mod, b_mod.reshape(depth, 1, n))


def _inproj_kernel(x_ref, mod_ref, w_ref, b_ref, o_ref, h_scr):
    @pl.when(pl.program_id(2) == 0)
    def _():
        m = mod_ref[0, 0]
        h = x_ref[0] * (1.0 + m[1:2]) + m[0:1]
        h_scr[...] = h.astype(BF16)

    o_ref[0] = jnp.dot(h_scr[...], w_ref[...], preferred_element_type=F32) + b_ref[...]


def _in_projection(x, modrows, w, bias, tn, name):
    b, tt, d = x.shape
    n = w.shape[1]
    return pl.pallas_call(
        _inproj_kernel,
        grid=(b, tt // ROW_TILE, n // tn),
        in_specs=[
            pl.BlockSpec((1, ROW_TILE, d), lambda i, r, j: (i, r, 0)),
            pl.BlockSpec((1, 1, 6, d), lambda i, r, j: (i, jnp.minimum(r, 1), 0, 0)),
            pl.BlockSpec((d, tn), lambda i, r, j: (0, j)),
            pl.BlockSpec((1, tn), lambda i, r, j: (0, j)),
        ],
        out_specs=pl.BlockSpec((1, ROW_TILE, tn), lambda i, r, j: (i, r, j)),
        out_shape=jax.ShapeDtypeStruct((b, tt, n), F32),
        scratch_shapes=[pltpu.VMEM((ROW_TILE, d), BF16)],
        compiler_params=_cparams(("parallel", "parallel", "arbitrary")),
        name=name,
    )(x, modrows, w, bias)


def _rope_tables(t_lat, n_ctx):
    n_rows = t_lat // GRID_W
    row = jnp.repeat(jnp.arange(n_rows, dtype=F32), GRID_W)
    col = jnp.tile(jnp.arange(GRID_W, dtype=F32), n_rows)
    quarter = QK_DIM // 4
    inv_freq = ROPE_BASE ** (-jnp.arange(quarter, dtype=F32) / quarter)
    ang = jnp.concatenate([row[:, None] * inv_freq, col[:, None] * inv_freq], -1)
    cos, sin = jnp.cos(ang), jnp.sin(ang)
    c64 = jnp.concatenate([cos, cos], -1)
    s64 = jnp.concatenate([-sin, sin], -1)
    c = jnp.concatenate([jnp.ones((n_ctx, QK_DIM), F32), c64], 0)
    s = jnp.concatenate([jnp.zeros((n_ctx, QK_DIM), F32), s64], 0)
    return jnp.tile(c, (1, LANES // QK_DIM)), jnp.tile(s, (1, LANES // QK_DIM))


def _rope(x, c, s):
    n = x.shape[-1]
    reps = n // LANES
    cf = c if reps == 1 else jnp.concatenate([c] * reps, -1)
    sf = s if reps == 1 else jnp.concatenate([s] * reps, -1)
    lane = lax.broadcasted_iota(jnp.int32, x.shape, 1)
    first_half = (lane % QK_DIM) < (QK_DIM // 2)
    swapped = jnp.where(first_half, pltpu.roll(x, n - QK_DIM // 2, 1), pltpu.roll(x, QK_DIM // 2, 1))
    return x * cf + swapped * sf


def _even_prep_kernel(q_ref, k_ref, v_ref, c_ref, s_ref, qo_ref, kt_ref, vo_ref):
    c, s = c_ref[...], s_ref[...]
    qo_ref[0] = (_rope(q_ref[0], c, s) * (QK_DIM ** -0.5)).astype(BF16)
    kt_ref[0] = _rope(k_ref[0], c, s).T.astype(BF16)
    vo_ref[0] = v_ref[0].astype(BF16)


def _even_prep(p, cos_t, sin_t):
    b, tt, _ = p.shape
    w = HEAD_A * 2 * QK_DIM
    blk = lambda j: pl.BlockSpec((1, ROW_TILE, w), lambda i, r, j=j: (i, r, j))
    tab = pl.BlockSpec((ROW_TILE, LANES), lambda i, r: (r, 0))
    return pl.pallas_call(
        _even_prep_kernel,
        grid=(b, tt // ROW_TILE),
        in_specs=[blk(0), blk(1), blk(2), tab, tab],
        out_specs=[
            pl.BlockSpec((1, ROW_TILE, w), lambda i, r: (i, r, 0)),
            pl.BlockSpec((1, w, ROW_TILE), lambda i, r: (i, 0, r)),
            pl.BlockSpec((1, ROW_TILE, w), lambda i, r: (i, r, 0)),
        ],
        out_shape=[
            jax.ShapeDtypeStruct((b, tt, w), BF16),
            jax.ShapeDtypeStruct((b, w, tt), BF16),
            jax.ShapeDtypeStruct((b, tt, w), BF16),
        ],
        compiler_params=_cparams(("parallel", "parallel")),
        name="diff_prep",
    )(p, p, p, cos_t, sin_t)


def _softmax_rows(s):
    m = jnp.max(s, axis=-1, keepdims=True)
    e = jnp.exp(s - m)
    return e / jnp.sum(e, axis=-1, keepdims=True)


def _diff_attn_kernel(lam_ref, g_ref, q_ref, kt_ref, v_ref, o_ref, *, lam_init):
    lp = lam_ref[...]
    a = jnp.sum(lp[0:1] * lp[1:2], axis=-1, keepdims=True)
    c = jnp.sum(lp[2:3] * lp[3:4], axis=-1, keepdims=True)
    lam = jnp.exp(a) - jnp.exp(c) + lam_init
    q = q_ref[0]
    lane = lax.broadcasted_iota(jnp.int32, q.shape, 1)
    zero = jnp.zeros_like(q)
    kt = kt_ref[0]
    p1 = _softmax_rows(jnp.dot(jnp.where(lane < QK_DIM, q, zero), kt, preferred_element_type=F32))
    p2 = _softmax_rows(jnp.dot(jnp.where(lane >= QK_DIM, q, zero), kt, preferred_element_type=F32))
    o = jnp.dot((p1 - lam * p2).astype(BF16), v_ref[0], preferred_element_type=F32)
    y = o * lax.rsqrt(jnp.mean(o * o, axis=-1, keepdims=True) + EPS) * g_ref[...]
    o_ref[0] = y * (1.0 - lam_init)


def _diff_attention(q, kt, v, lam_p, subln_g, lam_init):
    b, tt, w = q.shape
    dv = w // HEAD_A
    return pl.pallas_call(
        functools.partial(_diff_attn_kernel, lam_init=lam_init),
        grid=(b, HEAD_A, tt // ROW_TILE),
        in_specs=[
            pl.BlockSpec(lam_p.shape, lambda i, h, r: (0, 0)),
            pl.BlockSpec((1, dv), lambda i, h, r: (0, 0)),
            pl.BlockSpec((1, ROW_TILE, dv), lambda i, h, r: (i, r, h)),
            pl.BlockSpec((1, dv, tt), lambda i, h, r: (i, h, 0)),
            pl.BlockSpec((1, tt, dv), lambda i, h, r: (i, 0, h)),
        ],
        out_specs=pl.BlockSpec((1, ROW_TILE, dv), lambda i, h, r: (i, r, h)),
        out_shape=jax.ShapeDtypeStruct((b, tt, w), F32),
        compiler_params=_cparams(("parallel", "parallel", "arbitrary")),
        name="diff_attn",
    )(lam_p, subln_g.reshape(1, dv), q, kt, v)


def _mlstm_kernel(q_ref, k_ref, v_ref, og_ref, cwq_ref, cwk_ref, cbq_ref, cbk_ref, g_ref, mg_ref, o_ref,
                  qs_scr, kt_scr, va_scr, gate_scr, c_scr, *, n_ctx):
    tt, dk = q_ref.shape[1], q_ref.shape[2]
    nc = tt // CHUNK
    ncc = n_ctx // CHUNK
    ncp = g_ref.shape[2]
    h = pl.program_id(1)

    row = lax.broadcasted_iota(jnp.int32, (tt, dk), 0)
    first_rows = (row == 0) | (row == n_ctx)
    last_rows = (row == n_ctx - 1) | (row == tt - 1)

    def conv_silu(x, w, bias):
        xm = jnp.where(first_rows, 0.0, pltpu.roll(x, 1, 0))
        xp = jnp.where(last_rows, 0.0, pltpu.roll(x, tt - 1, 0))
        y = bias + xm * w[0:1] + x * w[1:2] + xp * w[2:3]
        return y * _sigmoid(y)

    qs_scr[...] = conv_silu(q_ref[0], cwq_ref[...], cbq_ref[...]).astype(BF16)
    kt_scr[...] = (conv_silu(k_ref[0], cwk_ref[...], cbk_ref[...]) * (dk ** -0.5)).T.astype(BF16)
    va_scr[:, 0:dk] = v_ref[0].astype(BF16)
    va_scr[:, dk:2 * dk] = jnp.ones((tt, dk), BF16)

    ri = lax.broadcasted_iota(jnp.int32, (CHUNK, CHUNK), 0)
    ci = lax.broadcasted_iota(jnp.int32, (CHUNK, CHUNK), 1)
    upper = (ri <= ci).astype(BF16)
    lower = (ri >= ci).astype(BF16)

    def cum(x, tri):
        hi, mid, lo = _split3(x)
        return (jnp.dot(hi, tri, preferred_element_type=F32) + jnp.dot(mid, tri, preferred_element_type=F32)
                + jnp.dot(lo, tri, preferred_element_type=F32))

    for d in range(2):
        li = g_ref[0, 8 * d + h]
        lf = _log_sigmoid(g_ref[0, 8 * d + 4 + h])
        bcum = cum(lf, upper if d == 0 else lower)
        b_end = bcum[:, CHUNK - 1:CHUNK] if d == 0 else bcum[:, 0:1]
        gate_scr[3 * d + 0] = bcum
        gate_scr[3 * d + 1] = b_end - bcum + li
        gate_scr[3 * d + 2] = li
    del ncp

    for d in range(2):
        c_scr[...] = jnp.zeros_like(c_scr)
        mask = (ri >= ci) if d == 0 else (ri <= ci)

        def body(i, m, d=d, mask=mask):
            if d == 0:
                c = i
            else:
                c = jnp.where(i < ncc, ncc - 1 - i, nc - 1 - (i - ncc))
            r0 = pl.multiple_of(c * CHUNK, CHUNK)
            qc = qs_scr[pl.ds(r0, CHUNK), :]
            ktc = kt_scr[:, pl.ds(r0, CHUNK)]
            vac = va_scr[pl.ds(r0, CHUNK), :]
            b_row = gate_scr[3 * d + 0, pl.ds(c, 1), :]
            le_row = gate_scr[3 * d + 1, pl.ds(c, 1), :]
            li_row = gate_scr[3 * d + 2, pl.ds(c, 1), :]
            b_end = b_row[:, CHUNK - 1:CHUNK] if d == 0 else b_row[:, 0:1]
            b_col = jnp.broadcast_to(b_row, (CHUNK, CHUNK)).T[:, 0:1]

            log_intra = jnp.where(mask, b_col - b_row + li_row, -jnp.inf)
            log_inter = b_col + m
            m_t = jnp.maximum(log_inter, jnp.max(log_intra, axis=-1, keepdims=True))
            w_intra = jnp.exp(log_intra - m_t)
            w_inter = jnp.exp(log_inter - m_t)
            s = jnp.dot(qc, ktc, preferred_element_type=F32) * w_intra
            qstate = jnp.dot(qc, c_scr[...].astype(BF16), preferred_element_type=F32)
            num = w_inter * qstate[:, 0:dk] + jnp.dot(s.astype(BF16), vac[:, 0:dk], preferred_element_type=F32)
            den = w_inter * qstate[:, dk:dk + 1] + jnp.sum(s, axis=-1, keepdims=True)
            hc = num / jnp.maximum(jnp.abs(den), jnp.exp(-m_t))
            if d == 0:
                o_ref[0, pl.ds(r0, CHUNK), :] = hc
            else:
                o_ref[0, pl.ds(r0, CHUNK), :] += hc

            m_new = jnp.maximum(b_end + m, jnp.max(le_row, axis=-1, keepdims=True))
            carry_w = jnp.exp(b_end + m - m_new)
            wk = jnp.exp(le_row - m_new)
            kw = (ktc.astype(F32) * wk).astype(BF16)
            c_scr[...] = carry_w * c_scr[...] + jnp.dot(kw, vac, preferred_element_type=F32)
            return m_new

        lax.fori_loop(0, nc, body, jnp.zeros((1, 1), F32))

    y = o_ref[0] * _sigmoid(og_ref[0])
    mu = jnp.mean(y, axis=-1, keepdims=True)
    var = jnp.mean(jnp.square(y - mu), axis=-1, keepdims=True)
    o_ref[0] = (y - mu) * lax.rsqrt(var + EPS) * mg_ref[...]


def _mlstm(p, gates4, conv_w, conv_b, mnorm_g, n_ctx):
    b, tt, _ = p.shape
    dk = 128
    nh = HEAD_B
    ncp = gates4.shape[2]
    colblk = lambda off: pl.BlockSpec((1, tt, dk), lambda i, h, off=off: (i, 0, off + h))
    return pl.pallas_call(
        functools.partial(_mlstm_kernel, n_ctx=n_ctx),
        grid=(b, nh),
        in_specs=[
            colblk(12), colblk(16), colblk(20), colblk(24),
            pl.BlockSpec((3, dk), lambda i, h: (0, h)),
            pl.BlockSpec((3, dk), lambda i, h: (0, nh + h)),
            pl.BlockSpec((1, dk), lambda i, h: (0, h)),
            pl.BlockSpec((1, dk), lambda i, h: (0, nh + h)),
            pl.BlockSpec((1, 16, ncp, CHUNK), lambda i, h: (i, 0, 0, 0)),
            pl.BlockSpec((1, dk), lambda i, h: (0, h)),
        ],
        out_specs=pl.BlockSpec((1, tt, dk), lambda i, h: (i, 0, h)),
        out_shape=jax.ShapeDtypeStruct((b, tt, nh * dk), F32),
        scratch_shapes=[
            pltpu.VMEM((tt, dk), BF16),
            pltpu.VMEM((dk, tt), BF16),
            pltpu.VMEM((tt, 2 * dk), BF16),
            pltpu.VMEM((6, ncp, CHUNK), F32),
            pltpu.VMEM((dk, 2 * dk), F32),
        ],
        compiler_params=_cparams(("parallel", "parallel")),
        name="mlstm",
    )(p, p, p, p, conv_w, conv_w, conv_b.reshape(1, -1), conv_b.reshape(1, -1), gates4, mnorm_g.reshape(1, -1))


def _retention_kernel(dl_ref, q_ref, k_ref, v_ref, g_ref, o_ref, kt_scr, s_scr, *, n_ctx):
    tt = q_ref.shape[1]
    dv = 128
    nc = tt // CHUNK
    ncc = n_ctx // CHUNK
    pair = pl.program_id(1)

    kt_scr[...] = (k_ref[0] * (QK_DIM ** -0.5)).T.astype(BF16)

    ri = lax.broadcasted_iota(jnp.int32, (CHUNK, CHUNK), 0)
    ci = lax.broadcasted_iota(jnp.int32, (CHUNK, CHUNK), 1)
    lane = lax.broadcasted_iota(jnp.int32, (CHUNK, 2 * QK_DIM), 1)
    pos_r = lax.broadcasted_iota(jnp.int32, (CHUNK, 1), 0).astype(F32)
    pos_c = lax.broadcasted_iota(jnp.int32, (1, CHUNK), 1).astype(F32)

    for d in range(2):
        decays = []
        for i in range(2):
            lg = _log_sigmoid(jnp.full((1, 1), dl_ref[d, 2 * pair + i], F32))
            if d == 0:
                lag = (ri - ci).astype(F32)
                intra = jnp.where(ri >= ci, jnp.exp(jnp.maximum(lag, 0.0) * lg), 0.0)
                q_dec = jnp.exp((pos_r + 1.0) * lg)
                k_dec = jnp.exp((CHUNK - 1.0 - pos_c) * lg)
            else:
                lag = (ci - ri).astype(F32)
                intra = jnp.where(ci >= ri, jnp.exp(jnp.maximum(lag, 0.0) * lg), 0.0)
                q_dec = jnp.exp((CHUNK - pos_r) * lg)
                k_dec = jnp.exp(pos_c * lg)
            decays.append((intra, q_dec, k_dec, jnp.exp(CHUNK * lg)))
        s_scr[...] = jnp.zeros_like(s_scr)

        def body(j, carry, d=d, decays=decays):
            if d == 0:
                c = j
            else:
                c = jnp.where(j < ncc, ncc - 1 - j, nc - 1 - (j - ncc))
            r0 = pl.multiple_of(c * CHUNK, CHUNK)
            qc = q_ref[0, pl.ds(r0, CHUNK), :]
            ktc = kt_scr[:, pl.ds(r0, CHUNK)]
            outs = []
            for i in range(2):
                intra, q_dec, k_dec, c_dec = decays[i]
                in_head = (lane >= QK_DIM * i) & (lane < QK_DIM * (i + 1))
                qz = jnp.where(in_head, qc, 0.0)
                vc = v_ref[0, pl.ds(r0, CHUNK), dv * i:dv * (i + 1)].astype(BF16)
                scores = jnp.dot(qz.astype(BF16), ktc, preferred_element_type=F32) * intra
                o = jnp.dot(scores.astype(BF16), vc, preferred_element_type=F32)
                o = o + jnp.dot((qz * q_dec).astype(BF16), s_scr[i].astype(BF16), preferred_element_type=F32)
                outs.append(o)
                kd = (ktc.astype(F32) * k_dec).astype(BF16)
                s_scr[i] = c_dec * s_scr[i] + jnp.dot(kd, vc, preferred_element_type=F32)
            oc = jnp.concatenate(outs, -1)
            if d == 0:
                o_ref[0, pl.ds(r0, CHUNK), :] = oc
            else:
                o_ref[0, pl.ds(r0, CHUNK), :] += oc
            return carry

        lax.fori_loop(0, nc, body, 0)

    for i in range(2):
        r = o_ref[0, :, dv * i:dv * (i + 1)]
        y = r * lax.rsqrt(jnp.mean(r * r, axis=-1, keepdims=True) + EPS)
        g = g_ref[0, :, dv * i:dv * (i + 1)]
        o_ref[0, :, dv * i:dv * (i + 1)] = y * (g * _sigmoid(g))


def _retention(p, decay_logit, n_ctx):
    b, tt, _ = p.shape
    return pl.pallas_call(
        functools.partial(_retention_kernel, n_ctx=n_ctx),
        grid=(b, HEAD_C // 2),
        in_specs=[
            pl.BlockSpec(memory_space=pltpu.SMEM),
            pl.BlockSpec((1, tt, 128), lambda i, j: (i, 0, j)),
            pl.BlockSpec((1, tt, 128), lambda i, j: (i, 0, 2 + j)),
            pl.BlockSpec((1, tt, 256), lambda i, j: (i, 0, 2 + j)),
            pl.BlockSpec((1, tt, 256), lambda i, j: (i, 0, 4 + j)),
        ],
        out_specs=pl.BlockSpec((1, tt, 256), lambda i, j: (i, 0, j)),
        out_shape=jax.ShapeDtypeStruct((b, tt, HEAD_C * 128), F32),
        scratch_shapes=[pltpu.VMEM((128, tt), BF16), pltpu.VMEM((2, 128, 128), F32)],
        compiler_params=_cparams(("parallel", "parallel")),
        name="retention",
    )(decay_logit, p, p, p, p)


def _gqa_prep_kernel(q_ref, k_ref, v_ref, gq_ref, gk_ref, bd_ref, c_ref, s_ref, qo_ref, kt_ref, vo_ref):
    c, s = c_ref[...], s_ref[...]
    bd = bd_ref[...]

    def head_rms(x, g):
        n = x.shape[-1]
        xx = x * x
        hi = xx.astype(BF16)
        lo = (xx - hi.astype(F32)).astype(BF16)
        blk = bd[0:n, 0:n]
        ss = jnp.dot(hi, blk, preferred_element_type=F32) + jnp.dot(lo, blk, preferred_element_type=F32)
        return x * lax.rsqrt(ss * (1.0 / QK_DIM) + EPS) * g

    qn = _rope(head_rms(q_ref[0], gq_ref[...]), c, s) * (QK_DIM ** -0.5)
    kn = _rope(head_rms(k_ref[0], gk_ref[...]), c, s)
    kt_ref[0] = kn.T.astype(BF16)
    v = v_ref[0]
    zeros = jnp.zeros((qn.shape[0], QK_DIM), F32)
    group = HEAD_D // HEAD_KV
    slabs = []
    for hq in range(HEAD_D):
        src = qn[:, QK_DIM * hq:QK_DIM * (hq + 1)]
        slabs.append(jnp.concatenate([src, zeros] if hq // group == 0 else [zeros, src], -1))
    qo_ref[0] = jnp.concatenate(slabs, -1).astype(BF16)
    v0, v1 = v[:, 0:QK_DIM], v[:, QK_DIM:2 * QK_DIM]
    vo_ref[0] = jnp.concatenate([v0, v0, v1, v1], -1).astype(BF16)


def _gqa_prep(p, qk_g, blockdiag, cos_t, sin_t):
    b, tt, _ = p.shape
    wq = HEAD_D * QK_DIM
    gq = jnp.tile(qk_g[0], HEAD_D).reshape(1, wq)
    gk = jnp.tile(qk_g[1], HEAD_KV).reshape(1, HEAD_KV * QK_DIM)
    tab = pl.BlockSpec((ROW_TILE, LANES), lambda i, r: (r, 0))
    return pl.pallas_call(
        _gqa_prep_kernel,
        grid=(b, tt // ROW_TILE),
        in_specs=[
            pl.BlockSpec((1, ROW_TILE, wq), lambda i, r: (i, r, 3)),
            pl.BlockSpec((1, ROW_TILE, 128), lambda i, r: (i, r, 16)),
            pl.BlockSpec((1, ROW_TILE, 128), lambda i, r: (i, r, 17)),
            pl.BlockSpec((1, wq), lambda i, r: (0, 0)),
            pl.BlockSpec((1, 128), lambda i, r: (0, 0)),
            pl.BlockSpec((wq, wq), lambda i, r: (0, 0)),
            tab, tab,
        ],
        out_specs=[
            pl.BlockSpec((1, ROW_TILE, HEAD_D * LANES), lambda i, r: (i, r, 0)),
            pl.BlockSpec((1, 128, ROW_TILE), lambda i, r: (i, 0, r)),
            pl.BlockSpec((1, ROW_TILE, 256), lambda i, r: (i, r, 0)),
        ],
        out_shape=[
            jax.ShapeDtypeStruct((b, tt, HEAD_D * LANES), BF16),
            jax.ShapeDtypeStruct((b, 128, tt), BF16),
            jax.ShapeDtypeStruct((b, tt, 256), BF16),
        ],
        compiler_params=_cparams(("parallel", "parallel")),
        name="gqa_prep",
    )(p, p, p, gq, gk, blockdiag, cos_t, sin_t)


def _gqa_attn_kernel(q_ref, kt_ref, v_ref, o_ref):
    q = q_ref[0]
    kt = kt_ref[0]
    v = v_ref[0]
    res = []
    for i in range(2):
        p = _softmax_rows(jnp.dot(q[:, LANES * i:LANES * (i + 1)], kt, preferred_element_type=F32))
        res.append(jnp.dot(p.astype(BF16), v, preferred_element_type=F32))
    lane = lax.broadcasted_iota(jnp.int32, res[0].shape, 1)
    o_ref[0] = jnp.where(lane < QK_DIM, res[0], res[1])


def _gqa_attention(qz, kt, v2):
    b, tt, _ = qz.shape
    npair = HEAD_D // 2
    return pl.pallas_call(
        _gqa_attn_kernel,
        grid=(b, npair, tt // ROW_TILE),
        in_specs=[
            pl.BlockSpec((1, ROW_TILE, 2 * LANES), lambda i, j, r: (i, r, j)),
            pl.BlockSpec((1, 128, tt), lambda i, j, r: (i, 0, 0)),
            pl.BlockSpec((1, tt, LANES), lambda i, j, r: (i, 0, j // (npair // HEAD_KV))),
        ],
        out_specs=pl.BlockSpec((1, ROW_TILE, LANES), lambda i, j, r: (i, r, j)),
        out_shape=jax.ShapeDtypeStruct((b, tt, HEAD_D * QK_DIM), F32),
        compiler_params=_cparams(("parallel", "parallel", "arbitrary")),
        name="gqa_attn",
    )(qz, kt, v2)


def _layer_norm(y, g, b):
    mu = jnp.mean(y, axis=-1, keepdims=True)
    var = jnp.mean(jnp.square(y - mu), axis=-1, keepdims=True)
    return (y - mu) * lax.rsqrt(var + EPS) * g + b


def _outproj_kernel(a_ref, m_ref, x_ref, mod_ref, w_ref, lng_ref, lnb_ref, wrh_ref, wrl_ref, rb_ref,
                    xo_ref, h_ref, lg_ref, *, alpha):
    half = a_ref.shape[2]
    o = jnp.dot(a_ref[0].astype(BF16), w_ref[0:half, :], preferred_element_type=F32)
    o = o + jnp.dot(m_ref[0].astype(BF16), w_ref[half:2 * half, :], preferred_element_type=F32)
    mod = mod_ref[0, 0]
    xn = _layer_norm(alpha * x_ref[0] + mod[2:3] * o, lng_ref[...], lnb_ref[...])
    xo_ref[0] = xn
    h = xn * (1.0 + mod[4:5]) + mod[3:4]
    hi = h.astype(BF16)
    lo = (h - hi.astype(F32)).astype(BF16)
    h_ref[0] = hi
    wh, wl = wrh_ref[...], wrl_ref[...]
    lg_ref[0] = (jnp.dot(hi, wh, preferred_element_type=F32) + jnp.dot(lo, wh, preferred_element_type=F32)
                 + jnp.dot(hi, wl, preferred_element_type=F32) + rb_ref[...])


def _out_projection(a, m, x, modrows, w_out, ln_g, ln_b, wr_hi, wr_lo, r_bias, alpha):
    b, tt, d = x.shape
    half = a.shape[2]
    row = lambda w: pl.BlockSpec((1, ROW_TILE, w), lambda i, r: (i, r, 0))
    full = lambda s: pl.BlockSpec(s, lambda i, r: (0,) * len(s))
    return pl.pallas_call(
        functools.partial(_outproj_kernel, alpha=alpha),
        grid=(b, tt // ROW_TILE),
        in_specs=[
            row(half), row(half), row(d),
            pl.BlockSpec((1, 1, 6, d), lambda i, r: (i, jnp.minimum(r, 1), 0, 0)),
            full((2 * half, d)), full((1, d)), full((1, d)),
            full((d, LANES)), full((d, LANES)), full((1, LANES)),
        ],
        out_specs=[row(d), row(d), row(LANES)],
        out_shape=[
            jax.ShapeDtypeStruct((b, tt, d), F32),
            jax.ShapeDtypeStruct((b, tt, d), BF16),
            jax.ShapeDtypeStruct((b, tt, LANES), F32),
        ],
        compiler_params=_cparams(("parallel", "parallel")),
        name="out_proj_norm_router",
    )(a, m, x, modrows, w_out, ln_g.reshape(1, d), ln_b.reshape(1, d), wr_hi, wr_lo, r_bias)


def _moe_kernel(be_ref, x_ref, wg_ref, wu_ref, wd_ref, o_ref):
    del be_ref
    x = x_ref[...]
    g = jnp.dot(x, wg_ref[0], preferred_element_type=F32)
    u = jnp.dot(x, wu_ref[0], preferred_element_type=F32)
    act = (g * _sigmoid(g)) * u
    o_ref[...] = jnp.dot(act.astype(BF16), wd_ref[0], preferred_element_type=F32)


def _moe_experts(buf, block_exp, w_gate, w_up, w_down):
    n_rows, d = buf.shape
    de = w_gate.shape[2]
    n_blocks = n_rows // MOE_BLOCK
    grid_spec = pltpu.PrefetchScalarGridSpec(
        num_scalar_prefetch=1,
        grid=(n_blocks,),
        in_specs=[
            pl.BlockSpec((MOE_BLOCK, d), lambda i, be: (i, 0)),
            pl.BlockSpec((1, d, de), lambda i, be: (be[i], 0, 0)),
            pl.BlockSpec((1, d, de), lambda i, be: (be[i], 0, 0)),
            pl.BlockSpec((1, de, d), lambda i, be: (be[i], 0, 0)),
        ],
        out_specs=pl.BlockSpec((MOE_BLOCK, d), lambda i, be: (i, 0)),
    )
    return pl.pallas_call(
        _moe_kernel,
        grid_spec=grid_spec,
        out_shape=jax.ShapeDtypeStruct((n_rows, d), F32),
        compiler_params=_cparams(("arbitrary",)),
        name="moe_experts",
    )(block_exp, buf, w_gate, w_up, w_down)


def _route(logits, n_tok):
    g_logits = logits[:, :N_GROUPS]
    g_prob = jax.nn.softmax(g_logits, -1)
    g_idx = jnp.argmax(g_logits, -1)
    g_w = jnp.take_along_axis(g_prob, g_idx[:, None], -1)
    e_logits = logits[:, N_GROUPS:N_GROUPS + N_EXPERTS].reshape(n_tok, N_GROUPS, EXP_PER_GROUP)
    e_logits = jnp.take_along_axis(e_logits, g_idx[:, None, None], 1)[:, 0]
    top_v, top_i = lax.top_k(e_logits, TOP_K)
    weights = jax.nn.softmax(top_v, -1) * g_w
    expert = (g_idx[:, None] * EXP_PER_GROUP + top_i).reshape(-1).astype(jnp.int32)
    n_slots = n_tok * TOP_K
    order = jnp.argsort(expert)
    s_exp = expert[order]
    counts = jnp.bincount(expert, length=N_EXPERTS)
    padded = (counts + MOE_BLOCK - 1) // MOE_BLOCK * MOE_BLOCK
    padded_end = jnp.cumsum(padded)
    dest = (padded_end - padded)[s_exp] + jnp.arange(n_slots) - (jnp.cumsum(counts) - counts)[s_exp]
    n_blocks = -(-n_slots // MOE_BLOCK) + N_EXPERTS
    block_exp = jnp.minimum(jnp.searchsorted(padded_end, jnp.arange(n_blocks) * MOE_BLOCK, side='right'),
                            N_EXPERTS - 1).astype(jnp.int32)
    src_tok = jnp.full((n_blocks * MOE_BLOCK,), n_tok, jnp.int32).at[dest].set((order // TOP_K).astype(jnp.int32))
    slot_dest = jnp.zeros((n_slots,), jnp.int32).at[order].set(dest.astype(jnp.int32))
    return weights, block_exp, src_tok, slot_dest


def _final_norm_kernel(x_ref, y_ref, mod_ref, lng_ref, lnb_ref, o_ref, *, alpha):
    mod = mod_ref[0, 0]
    o_ref[0] = _layer_norm(alpha * x_ref[0] + mod[5:6] * y_ref[0], lng_ref[...], lnb_ref[...])


def _final_norm(x, y, modrows, ln_g, ln_b, alpha):
    b, tt, d = x.shape
    row = pl.BlockSpec((1, ROW_TILE, d), lambda i, r: (i, r, 0))
    full = pl.BlockSpec((1, d), lambda i, r: (0, 0))
    return pl.pallas_call(
        functools.partial(_final_norm_kernel, alpha=alpha),
        grid=(b, tt // ROW_TILE),
        in_specs=[row, row, pl.BlockSpec((1, 1, 6, d), lambda i, r: (i, jnp.minimum(r, 1), 0, 0)), full, full],
        out_specs=row,
        out_shape=jax.ShapeDtypeStruct((b, tt, d), F32),
        compiler_params=_cparams(("parallel", "parallel")),
        name="moe_residual_norm",
    )(x, y, modrows, ln_g.reshape(1, d), ln_b.reshape(1, d))


def kernel(x, c, ctx, c_ctx, w_mod, b_mod, ln_g, ln_b, ev_w_in, ev_w_out, ev_lambda, ev_subln_g, ev_conv_w,
           ev_conv_b, ev_gate_b, ev_mnorm_g, od_w_in, od_w_out, od_decay, od_qk_g, moe_w_group, moe_b_group,
           moe_w_router, moe_b_router, moe_w_gate, moe_w_up, moe_w_down):
    bsz, t_lat, d = x.shape
    n_ctx = ctx.shape[1]
    depth = w_mod.shape[0]
    tt = n_ctx + t_lat
    n_tok = bsz * tt
    alpha = (2 * depth) ** 0.25
    assert n_ctx == ROW_TILE and t_lat % ROW_TILE == 0 and bsz < 8

    xs = jnp.concatenate([ctx, x], axis=1)

    c_rows = jnp.zeros((8, d), F32).at[:bsz].set(c).at[bsz].set(c_ctx)
    mods = _modulation(c_rows, w_mod, b_mod)
    mod_lat = mods[:, :bsz].reshape(depth, bsz, 6, d)
    mod_ctx = jnp.broadcast_to(mods[:, bsz].reshape(depth, 1, 6, d), (depth, bsz, 6, d))
    modrows = jnp.stack([mod_ctx, mod_lat], axis=2)

    cos_t, sin_t = _rope_tables(t_lat, n_ctx)
    gi = np.arange(HEAD_D * QK_DIM) // QK_DIM
    blockdiag = jnp.asarray(gi[:, None] == gi[None, :], BF16)
    zero_bias = lambda n: jnp.zeros((1, n), F32)
    n_main_even = ev_w_in.shape[2] - 4 * HEAD_B
    ncp = -(-(tt // CHUNK) // 8) * 8

    for layer in range(depth):
        i = layer // 2
        mr = modrows[layer]
        if layer % 2 == 0:
            w_in = ev_w_in[i]
            p = _in_projection(xs, mr, w_in[:, :n_main_even].astype(BF16), zero_bias(n_main_even), 512, "in_proj_even")
            w_gates = jnp.zeros((d, LANES), F32).at[:, :4 * HEAD_B].set(w_in[:, n_main_even:]).astype(BF16)
            b_gates = jnp.zeros((1, LANES), F32).at[0, :4 * HEAD_B].set(ev_gate_b[i].reshape(-1))
            gates = _in_projection(xs, mr, w_gates, b_gates, LANES, "in_proj_gates")
            gates4 = jnp.swapaxes(gates[:, :, :4 * HEAD_B], 1, 2).reshape(bsz, 4 * HEAD_B, tt // CHUNK, CHUNK)
            gates4 = jnp.pad(gates4, ((0, 0), (0, 0), (0, ncp - tt // CHUNK), (0, 0)))
            lam_init = 0.8 - 0.6 * math.exp(-0.3 * layer)
            q_r, k_t, v_b = _even_prep(p, cos_t, sin_t)
            mix_a = _diff_attention(q_r, k_t, v_b, ev_lambda[i], ev_subln_g[i], lam_init)
            mix_b = _mlstm(p, gates4, ev_conv_w[i], ev_conv_b[i], ev_mnorm_g[i], n_ctx)
            w_out = ev_w_out[i]
        else:
            p = _in_projection(xs, mr, od_w_in[i].astype(BF16), zero_bias(od_w_in.shape[2]), 768, "in_proj_odd")
            mix_a = _retention(p, od_decay[i], n_ctx)
            qz, k_t, v2 = _gqa_prep(p, od_qk_g[i], blockdiag, cos_t, sin_t)
            mix_b = _gqa_attention(qz, k_t, v2)
            w_out = od_w_out[i]

        w_r = jnp.zeros((d, LANES), F32).at[:, :N_GROUPS].set(moe_w_group[layer])
        w_r = w_r.at[:, N_GROUPS:N_GROUPS + N_EXPERTS].set(moe_w_router[layer])
        wr_hi = w_r.astype(BF16)
        wr_lo = (w_r - wr_hi.astype(F32)).astype(BF16)
        r_bias = jnp.zeros((1, LANES), F32).at[0, :N_GROUPS].set(moe_b_group[layer])
        r_bias = r_bias.at[0, N_GROUPS:N_GROUPS + N_EXPERTS].set(moe_b_router[layer])
        x_mid, h_moe, logits = _out_projection(mix_a, mix_b, xs, mr, w_out.astype(BF16), ln_g[layer, 0],
                                               ln_b[layer, 0], wr_hi, wr_lo, r_bias, alpha)

        weights, block_exp, src_tok, slot_dest = _route(logits.reshape(n_tok, LANES), n_tok)
        h_pad = jnp.concatenate([h_moe.reshape(n_tok, d), jnp.zeros((1, d), BF16)], 0)
        buf = jnp.take(h_pad, src_tok, axis=0)
        y_blocks = _moe_experts(buf, block_exp, moe_w_gate[layer].astype(BF16), moe_w_up[layer].astype(BF16),
                                moe_w_down[layer].astype(BF16))
        y_slots = jnp.take(y_blocks, slot_dest, axis=0).reshape(n_tok, TOP_K, d)
        y = jnp.sum(y_slots * weights[:, :, None], axis=1).reshape(bsz, tt, d)
        xs = _final_norm(x_mid, y, mr, ln_g[layer, 1], ln_b[layer, 1], alpha)

    return xs[:, n_ctx:]
```
